```python
import math
import jax
import jax.numpy as jnp
from jax import lax
import numpy as np

D_MODEL = 1024
BATCH = 8
SEQ = 2048
DEPTH = 1
DEC_BATCH = 128
DEC_SEQ = 8
PAST_LEN = 8192
PAGE_SIZE = 128

D_MIX = D_MODEL
ATTN_WIDTH = D_MIX // 2
CONV_WIDTH = D_MIX - ATTN_WIDTH
HEAD_DIM = 64
N_HEADS = ATTN_WIDTH // HEAD_DIM
CONV_SPAN = 31
CONV_CTX = CONV_SPAN - 1
MOBA_BLOCK = 256
MOBA_TOPK = 3
Q_CHUNK = 128
N_BUCKETS = 32
MAX_DISTANCE = 128
EPS = 1e-6
NEG = -1e30
IN_SPLITS = (ATTN_WIDTH, 2 * ATTN_WIDTH, 3 * ATTN_WIDTH, 4 * ATTN_WIDTH,
             4 * ATTN_WIDTH + CONV_WIDTH, 4 * ATTN_WIDTH + 2 * CONV_WIDTH)
D_IN = 4 * ATTN_WIDTH + 3 * CONV_WIDTH

kernel_name = 'hymba_moba_conformer_conv_step'


def rms_norm(x, g):
    xf = x.astype(jnp.float32)
    y = xf * lax.rsqrt(jnp.mean(xf * xf, axis=-1, keepdims=True) + EPS)
    return (y * g.astype(jnp.float32)).astype(x.dtype)


def layer_norm(x, g, b):
    xf = x.astype(jnp.float32)
    xc = xf - jnp.mean(xf, axis=-1, keepdims=True)
    var = jnp.mean(xc * xc, axis=-1, keepdims=True)
    y = xc * lax.rsqrt(var + EPS) * g.astype(jnp.float32) + b.astype(jnp.float32)
    return y.astype(x.dtype)


def rel_bucket(dist):
    n = jnp.maximum(dist, 0)
    exact = N_BUCKETS // 2
    nf = jnp.maximum(n, 1).astype(jnp.float32)
    large = exact + (jnp.log(nf / exact) / math.log(MAX_DISTANCE / exact)
                     * (N_BUCKETS - exact)).astype(jnp.int32)
    return jnp.where(n < exact, n, jnp.minimum(large, N_BUCKETS - 1))


def moba_chunk(q, qpos, kbh, vbh, k_means, rel_bias):
    nb = kbh.shape[0]
    n_q = q.shape[0]
    topk = min(MOBA_TOPK, nb)
    scale = HEAD_DIM ** -0.5
    qf = q.astype(jnp.float32)
    qblk = qpos // MOBA_BLOCK
    own = qblk[0]
    gate = jnp.einsum('qhd,nhd->qhn', qf, k_means)
    fully_past = jnp.arange(nb)[None, None, :] < qblk[:, None, None]
    gate = jnp.where(fully_past, gate, NEG)
    _, sel = lax.top_k(gate, topk)
    sel_ok = sel < qblk[:, None, None]
    hidx = jnp.arange(N_HEADS)[None, :, None]
    k_sel = kbh[sel, hidx].astype(jnp.float32)
    v_sel = vbh[sel, hidx].astype(jnp.float32)
    kpos_sel = sel[..., None] * MOBA_BLOCK + jnp.arange(MOBA_BLOCK)
    b_sel = rel_bias[rel_bucket(qpos[:, None, None, None] - kpos_sel), hidx[..., None]]
    s_sel = jnp.einsum('qhd,qhjkd->qhjk', qf, k_sel) * scale + b_sel
    s_sel = jnp.where(sel_ok[..., None], s_sel, NEG)
    k_own = lax.dynamic_index_in_dim(kbh, own, axis=0, keepdims=False).astype(jnp.float32)
    v_own = lax.dynamic_index_in_dim(vbh, own, axis=0, keepdims=False).astype(jnp.float32)
    dist = qpos[:, None] - (own * MOBA_BLOCK + jnp.arange(MOBA_BLOCK))[None, :]
    b_own = jnp.transpose(rel_bias[rel_bucket(dist)], (0, 2, 1))
    s_own = jnp.einsum('qhd,hkd->qhk', qf, k_own) * scale + b_own
    s_own = jnp.where((dist >= 0)[:, None, :], s_own, NEG)
    logits = jnp.concatenate([s_sel.reshape(n_q, N_HEADS, topk * MOBA_BLOCK), s_own], axis=-1)
    p = jax.nn.softmax(logits, axis=-1)
    p_sel = p[..., :topk * MOBA_BLOCK].reshape(n_q, N_HEADS, topk, MOBA_BLOCK)
    p_own = p[..., topk * MOBA_BLOCK:]
    out = jnp.einsum('qhjk,qhjkd->qhd', p_sel, v_sel) + jnp.einsum('qhk,hkd->qhd', p_own, v_own)
    return out.astype(q.dtype)


def moba_sequence(q, k, v, qpos, rel_bias):
    n_k = k.shape[0]
    nb = -(-n_k // MOBA_BLOCK)
    pad = nb * MOBA_BLOCK - n_k

    def blocks(t):
        t = jnp.pad(t, ((0, pad), (0, 0), (0, 0)))
        return t.reshape(nb, MOBA_BLOCK, N_HEADS, HEAD_DIM).transpose(0, 2, 1, 3)

    kbh, vbh = blocks(k), blocks(v)
    k_means = jnp.mean(kbh.astype(jnp.float32), axis=2)
    n_q = q.shape[0]
    chunk = min(Q_CHUNK, n_q)
    n_c = n_q // chunk
    out = lax.map(lambda a: moba_chunk(a[0], a[1], kbh, vbh, k_means, rel_bias),
                  (q.reshape(n_c, chunk, N_HEADS, HEAD_DIM), qpos.reshape(n_c, chunk)))
    return out.reshape(n_q, N_HEADS, HEAD_DIM)


def modulate(x, c, w_ada, b_ada, g_norm):
    mod = jnp.einsum('bd,de->be', c, w_ada) + b_ada
    shift, scale, gate = jnp.split(mod, 3, axis=-1)
    h = rms_norm(x, g_norm) * (1 + scale[:, None, :]) + shift[:, None, :]
    return h, gate


def in_projection(h, w_in):
    z = jnp.einsum('bsd,de->bse', h, w_in)
    q, k, v, z_attn, a, g, z_conv = jnp.split(z, IN_SPLITS, axis=-1)
    b, s = h.shape[:2]
    heads = lambda t: t.reshape(b, s, N_HEADS, HEAD_DIM)
    return heads(q), heads(k), heads(v), z_attn, a * jax.nn.sigmoid(g), z_conv


def causal_dwconv(u_ext, w, b):
    y = lax.conv_general_dilated(u_ext, w[:, None, :], window_strides=(1,), padding='VALID',
                                 dimension_numbers=('NWC', 'WIO', 'NWC'),
                                 feature_group_count=CONV_WIDTH)
    return y + b


def merge_heads(o_attn, z_attn, conv_out, z_conv, ln_g, ln_b, w_out):
    b, s = z_attn.shape[:2]
    y_attn = o_attn.reshape(b, s, ATTN_WIDTH) * jax.nn.silu(z_attn)
    y_conv = jax.nn.silu(layer_norm(conv_out, ln_g, ln_b)) * jax.nn.silu(z_conv)
    return jnp.einsum('bse,ed->bsd', jnp.concatenate([y_attn, y_conv], axis=-1), w_out)


def setup_inputs(seed: int = 0) -> dict:
    key = jax.random.key(seed)
    ks = jax.random.split(key, 24)
    f32 = jnp.float32
    n_pages = PAST_LEN // PAGE_SIZE
    n_pool = (DEC_BATCH * n_pages * 5) // 4
    nrm = lambda k, shape, s: jax.random.normal(k, shape, f32) * s
    page_table = jax.random.permutation(ks[7], n_pool)[:DEC_BATCH * n_pages]
    page_table = page_table.reshape(DEC_BATCH, n_pages).astype(jnp.int32)
    return {
        'x_prompt': nrm(ks[0], (BATCH, SEQ, D_MODEL), 1.0),
        'x_sample': nrm(ks[1], (DEC_BATCH, DEC_SEQ, D_MODEL), 1.0),
        'cache_k': nrm(ks[2], (DEPTH, n_pool, PAGE_SIZE, N_HEADS, HEAD_DIM), 1.0),
        'cache_v': nrm(ks[3], (DEPTH, n_pool, PAGE_SIZE, N_HEADS, HEAD_DIM), 1.0),
        'state_conv': nrm(ks[4], (DEPTH, DEC_BATCH, CONV_CTX, CONV_WIDTH), 0.5),
        'page_table': page_table,
        'c_prompt': nrm(ks[5], (BATCH, D_MODEL), 1.0),
        'c_sample': nrm(ks[6], (DEC_BATCH, D_MODEL), 1.0),
        'w_ada': nrm(ks[8], (DEPTH, D_MODEL, 3 * D_MODEL), 0.5 * D_MODEL ** -0.5),
        'b_ada': nrm(ks[9], (DEPTH, 3 * D_MODEL), 0.01),
        'g_norm': 1.0 + nrm(ks[10], (DEPTH, D_MODEL), 0.01),
        'w_in': nrm(ks[11], (DEPTH, D_MODEL, D_IN), D_MODEL ** -0.5),
        'conv_w': nrm(ks[12], (DEPTH, CONV_SPAN, CONV_WIDTH), CONV_SPAN ** -0.5),
        'conv_b': nrm(ks[13], (DEPTH, CONV_WIDTH), 0.01),
        'ln_g': 1.0 + nrm(ks[14], (DEPTH, CONV_WIDTH), 0.01),
        'ln_b': nrm(ks[15], (DEPTH, CONV_WIDTH), 0.01),
        'w_out': nrm(ks[16], (DEPTH, D_MIX, D_MODEL), D_MIX ** -0.5),
        'rel_bias': nrm(ks[17], (N_BUCKETS, N_HEADS), 0.5),
        'g_final': 1.0 + nrm(ks[18], (D_MODEL,), 0.01),
    }


def reference(x_prompt, x_sample, cache_k, cache_v, state_conv, page_table, c_prompt, c_sample,
              w_ada, b_ada, g_norm, w_in, conv_w, conv_b, ln_g, ln_b, w_out, rel_bias, g_final):
    pos_p = jnp.arange(x_prompt.shape[1], dtype=jnp.int32)
    n_past = page_table.shape[1] * cache_k.shape[2]
    pos_s = n_past + jnp.arange(x_sample.shape[1], dtype=jnp.int32)
    xp, xs = x_prompt, x_sample
    kp_rows, vp_rows, cp_states, ks_rows, vs_rows, cs_states = [], [], [], [], [], []
    for l in range(DEPTH):
        h, gate = modulate(xp, c_prompt, w_ada[l], b_ada[l], g_norm[l])
        q, k, v, z_attn, u, z_conv = in_projection(h, w_in[l])
        o = lax.map(lambda a: moba_sequence(a[0], a[1], a[2], pos_p, rel_bias), (q, k, v))
        u_ext = jnp.pad(u, ((0, 0), (CONV_CTX, 0), (0, 0)))
        conv_out = causal_dwconv(u_ext, conv_w[l], conv_b[l])
        xp = xp + gate[:, None, :] * merge_heads(o, z_attn, conv_out, z_conv, ln_g[l], ln_b[l], w_out[l])
        kp_rows.append(k)
        vp_rows.append(v)
        cp_states.append(u_ext[:, -CONV_CTX:])

        h, gate = modulate(xs, c_sample, w_ada[l], b_ada[l], g_norm[l])
        q, k, v, z_attn, u, z_conv = in_projection(h, w_in[l])

        def sample_seq(a, l=l):
            q_b, k_b, v_b, pt_b = a
            past_k = cache_k[l, pt_b].reshape(-1, N_HEADS, HEAD_DIM)
            past_v = cache_v[l, pt_b].reshape(-1, N_HEADS, HEAD_DIM)
            k_all = jnp.concatenate([past_k, k_b.astype(past_k.dtype)], axis=0)
            v_all = jnp.concatenate([past_v, v_b.astype(past_v.dtype)], axis=0)
            return moba_sequence(q_b, k_all, v_all, pos_s, rel_bias)

        o = lax.map(sample_seq, (q, k, v, page_table))
        u_ext = jnp.concatenate([state_conv[l], u.astype(state_conv.dtype)], axis=1)
        conv_out = causal_dwconv(u_ext, conv_w[l], conv_b[l])
        xs = xs + gate[:, None, :] * merge_heads(o, z_attn, conv_out, z_conv, ln_g[l], ln_b[l], w_out[l])
        ks_rows.append(k)
        vs_rows.append(v)
        cs_states.append(u_ext[:, -CONV_CTX:])
    y_prompt = rms_norm(xp, g_final)
    y_sample = rms_norm(xs, g_final)
    return (y_prompt, y_sample, jnp.stack(kp_rows), jnp.stack(vp_rows), jnp.stack(cp_states),
            jnp.stack(ks_rows), jnp.stack(vs_rows), jnp.stack(cs_states))
```

```python
import functools
import math

import numpy as np
import jax
import jax.numpy as jnp
from jax import lax
from jax.experimental import pallas as pl
from jax.experimental.pallas import tpu as pltpu

HEAD_DIM = 64
CONV_SPAN = 31
CONV_CTX = CONV_SPAN - 1
MOBA_BLOCK = 256
MOBA_TOPK = 3
Q_CHUNK = 128
N_BUCKETS = 32
MAX_DISTANCE = 128
EPS = 1e-6
NEG = -1e30

LANES = 128
SUBLANES = 8
VMEM_LIMIT = 56 * 1024 * 1024

_NT = (((1,), (1,)), ((), ()))

f32 = jnp.float32
bf16 = jnp.bfloat16


def _silu(x):
    return x * jax.nn.sigmoid(x)


def _ada_kernel(c_ref, w_ref, b_ref, o_ref):
    acc = jnp.dot(c_ref[...].astype(bf16), w_ref[...].astype(bf16), preferred_element_type=f32)
    o_ref[...] = acc + b_ref[...]


def _ada(c_all, w_ada, b_ada):
    n, d = c_all.shape
    e = w_ada.shape[1]
    tn = 1024
    return pl.pallas_call(
        _ada_kernel,
        out_shape=jax.ShapeDtypeStruct((n, e), f32),
        grid=(e // tn,),
        in_specs=[pl.BlockSpec((n, d), lambda j: (0, 0)),
                  pl.BlockSpec((d, tn), lambda j: (0, j)),
                  pl.BlockSpec((1, tn), lambda j: (0, j))],
        out_specs=pl.BlockSpec((n, tn), lambda j: (0, j)),
        compiler_params=pltpu.CompilerParams(dimension_semantics=("arbitrary",),
                                             vmem_limit_bytes=VMEM_LIMIT),
        name="ada",
    )(c_all, w_ada, b_ada.reshape(1, e))


def _bucket_np(dist):
    dist = np.asarray(dist, np.int64)
    n = np.maximum(dist, 0)
    exact = N_BUCKETS // 2
    nf = np.maximum(n, 1).astype(np.float32)
    large = exact + (np.log(nf / np.float32(exact)) / np.float32(math.log(MAX_DISTANCE / exact))
                     * np.float32(N_BUCKETS - exact)).astype(np.int32)
    b = np.where(n < exact, n, np.minimum(large, N_BUCKETS - 1))
    return np.where(dist < 0, -1, b).astype(np.int32)


def _bias_kernel(rb_ref, idx_ref, o_ref):
    h = pl.program_id(0)
    idx = idx_ref[...]
    acc = jnp.full(idx.shape, NEG, f32)
    for b in range(N_BUCKETS):
        acc = jnp.where(idx == b, rb_ref[b, h], acc)
    o_ref[0] = acc


def _bias_tiles(rel_bias, idx):
    n_heads = rel_bias.shape[1]
    r, c = idx.shape
    return pl.pallas_call(
        _bias_kernel,
        out_shape=jax.ShapeDtypeStruct((n_heads, r, c), f32),
        grid=(n_heads,),
        in_specs=[pl.BlockSpec(memory_space=pltpu.SMEM),
                  pl.BlockSpec((r, c), lambda h: (0, 0))],
        out_specs=pl.BlockSpec((1, r, c), lambda h: (h, 0, 0)),
        compiler_params=pltpu.CompilerParams(dimension_semantics=("arbitrary",)),
        name="bias",
    )(rel_bias, jnp.asarray(idx))


def _in_proj_kernel(x_ref, sc_ref, sh_ref, g_ref, w_ref,
                    q_ref, k_ref, v_ref, za_ref, u_ref, zc_ref, *, width):
    g_sz, tm, d = x_ref.shape
    x = x_ref[...]
    ms = jnp.mean(x * x, axis=-1, keepdims=True)
    y = x * lax.rsqrt(ms + EPS) * g_ref[...]
    h = y * (1.0 + sc_ref[...]) + sh_ref[...]
    hb = h.reshape(g_sz * tm, d).astype(bf16)

    def col(j):
        z = jnp.dot(hb, w_ref[:, j * width:(j + 1) * width], preferred_element_type=f32)
        return z.reshape(g_sz, tm, width)

    q_ref[...] = col(0)
    k_ref[...] = col(1)
    v_ref[...] = col(2)
    za_ref[...] = _silu(col(3))
    u_ref[...] = col(4) * jax.nn.sigmoid(col(5))
    zc_ref[...] = _silu(col(6))


def _in_proj(x, scale, shift, g_norm, w_in_b, *, g_sz, tm):
    n_b, s, d = x.shape
    width = w_in_b.shape[1] // 7
    grid = (n_b // g_sz, s // tm)
    row = lambda b, i: (b, i, 0)
    per_b = lambda b, i: (b, 0, 0)
    out_sd = jax.ShapeDtypeStruct((n_b, s, width), f32)
    return pl.pallas_call(
        functools.partial(_in_proj_kernel, width=width),
        out_shape=(out_sd,) * 6,
        grid=grid,
        in_specs=[pl.BlockSpec((g_sz, tm, d), row),
                  pl.BlockSpec((g_sz, 1, d), per_b),
                  pl.BlockSpec((g_sz, 1, d), per_b),
                  pl.BlockSpec((1, d), lambda b, i: (0, 0)),
                  pl.BlockSpec(w_in_b.shape, lambda b, i: (0, 0))],
        out_specs=(pl.BlockSpec((g_sz, tm, width), row),) * 6,
        compiler_params=pltpu.CompilerParams(dimension_semantics=("arbitrary", "arbitrary"),
                                             vmem_limit_bytes=VMEM_LIMIT),
        name="in_proj",
    )(x, scale.reshape(n_b, 1, d), shift.reshape(n_b, 1, d), g_norm.reshape(1, d), w_in_b)


def _p_attn_kernel(q_ref, k_ref, v_ref, bias_ref, o_ref, kaug, vaug, kmean, s_scr, *, n_blk):
    c = pl.program_id(2)
    seq = k_ref.shape[1]
    pair_w = k_ref.shape[2]
    chunks_per_blk = MOBA_BLOCK // Q_CHUNK

    @pl.when(c == 0)
    def _build():
        k = k_ref[0]
        kaug[:, :pair_w] = k.astype(bf16)
        row_blk = lax.broadcasted_iota(jnp.int32, (seq, LANES), 0) // MOBA_BLOCK
        lane = lax.broadcasted_iota(jnp.int32, (seq, LANES), 1)
        kaug[:, pair_w:] = jnp.where(row_blk == lane, 1.0, 0.0).astype(bf16)
        vaug[:, :pair_w] = v_ref[0].astype(bf16)
        vaug[:, pair_w:] = jnp.where(lane == 0, 1.0, 0.0).astype(bf16)
        kmean[...] = jnp.mean(k.reshape(n_blk, MOBA_BLOCK, pair_w), axis=1)

    qblk = c // chunks_per_blk
    half = c % chunks_per_blk
    q = q_ref[0]
    lane_q = lax.broadcasted_iota(jnp.int32, q.shape, 1)
    blk_id = lax.broadcasted_iota(jnp.int32, (n_blk, Q_CHUNK), 0)
    out = jnp.zeros(q.shape, f32)

    for hh in range(pair_w // HEAD_DIM):
        in_head = (lane_q // HEAD_DIM) == hh
        qm = jnp.where(in_head, q, 0.0)
        qmb = qm.astype(bf16)
        gate = lax.dot_general(kmean[...].astype(bf16), qmb, _NT, preferred_element_type=f32)
        valid = blk_id < qblk
        gv = jnp.where(valid, gate, NEG)
        rank = jnp.zeros(gv.shape, jnp.int32)
        for m in range(n_blk):
            row_m = jnp.broadcast_to(gv[m:m + 1, :], gv.shape)
            beats = (row_m > gv) | ((row_m == gv) & (m < blk_id))
            rank = rank + beats.astype(jnp.int32)
        allowed = ((rank < MOBA_TOPK) & valid) | (blk_id == qblk)
        selneg = jnp.where(allowed, 0.0, NEG)
        selneg_sq = jnp.concatenate([selneg, jnp.zeros((Q_CHUNK - n_blk, Q_CHUNK), f32)], axis=0)
        q_aug = jnp.concatenate([(qm * HEAD_DIM ** -0.5).astype(bf16), selneg_sq.T.astype(bf16)], axis=1)

        def tile_of(j):
            return jnp.where(j == qblk, half,
                             jnp.where((j == qblk - 1) & (half == 0), 2, 3))

        def pass1(j, mrun):
            kj = kaug[pl.ds(pl.multiple_of(j * MOBA_BLOCK, MOBA_BLOCK), MOBA_BLOCK), :]
            s = lax.dot_general(q_aug, kj, _NT, preferred_element_type=f32)
            s = s + bias_ref[hh, tile_of(j)]
            s_scr[j] = s
            return jnp.maximum(mrun, jnp.maximum(s[:, :LANES], s[:, LANES:]))

        mrun = lax.fori_loop(0, qblk + 1, pass1, jnp.full((Q_CHUNK, LANES), -jnp.inf, f32))
        m_row = jnp.max(mrun, axis=1, keepdims=True)

        def pass2(j, acc):
            p = jnp.exp(s_scr[j] - m_row).astype(bf16)
            vj = vaug[pl.ds(pl.multiple_of(j * MOBA_BLOCK, MOBA_BLOCK), MOBA_BLOCK), :]
            return acc + jnp.dot(p, vj, preferred_element_type=f32)

        acc = lax.fori_loop(0, qblk + 1, pass2, jnp.zeros((Q_CHUNK, 2 * LANES), f32))
        o_h = acc[:, :pair_w] / acc[:, pair_w:pair_w + 1]
        out = jnp.where(in_head, o_h, out)

    o_ref[0] = out


def _p_attn(q, k, v, bias_p):
    n_b, seq, width = q.shape
    n_blk = seq // MOBA_BLOCK
    n_pairs = width // LANES
    heads_per_pair = LANES // HEAD_DIM
    n_tiles = bias_p.shape[1]
    grid = (n_b, n_pairs, seq // Q_CHUNK)
    return pl.pallas_call(
        functools.partial(_p_attn_kernel, n_blk=n_blk),
        out_shape=jax.ShapeDtypeStruct((n_b, seq, width), f32),
        grid=grid,
        in_specs=[pl.BlockSpec((1, Q_CHUNK, LANES), lambda b, p, c: (b, c, p)),
                  pl.BlockSpec((1, seq, LANES), lambda b, p, c: (b, 0, p)),
                  pl.BlockSpec((1, seq, LANES), lambda b, p, c: (b, 0, p)),
                  pl.BlockSpec((heads_per_pair, n_tiles, Q_CHUNK, MOBA_BLOCK), lambda b, p, c: (p, 0, 0, 0))],
        out_specs=pl.BlockSpec((1, Q_CHUNK, LANES), lambda b, p, c: (b, c, p)),
        scratch_shapes=[pltpu.VMEM((seq, 2 * LANES), bf16),
                        pltpu.VMEM((seq, 2 * LANES), bf16),
                        pltpu.VMEM((n_blk, LANES), f32),
                        pltpu.VMEM((n_blk, Q_CHUNK, MOBA_BLOCK), f32)],
        compiler_params=pltpu.CompilerParams(dimension_semantics=("arbitrary", "arbitrary", "arbitrary"),
                                             vmem_limit_bytes=VMEM_LIMIT),
        name="p_attn",
    )(q, k, v, bias_p)


S_NBUF = 4
S_AHEAD = S_NBUF - 1
PAGES_PER_BLK = 2


def _s_attn_kernel(pt_ref, q_ref, kn_ref, vn_ref, near_ref, ownb_ref, c31_ref, ck_hbm, cv_hbm,
                   o_ref, buf, sem, s_scr, *, n_seq, n_pages, n_past_blk, page):
    b = pl.program_id(0)
    t_q, width = q_ref.shape[1], q_ref.shape[2]
    n_heads = width // HEAD_DIM
    rows = n_heads * t_q

    def start_blk(src_hbm, sq, blk, slot):
        for pg in range(PAGES_PER_BLK):
            pid = pt_ref[sq * n_pages + blk * PAGES_PER_BLK + pg]
            pltpu.make_async_copy(src_hbm.at[pid], buf.at[slot, pg], sem.at[slot]).start()

    def wait_blk(slot):
        for pg in range(PAGES_PER_BLK):
            pltpu.make_async_copy(ck_hbm.at[0], buf.at[slot, pg], sem.at[slot]).wait()

    @pl.when(b == 0)
    def _prologue():
        for i in range(S_AHEAD):
            start_blk(ck_hbm, 0, i, i % S_NBUF)

    q = q_ref[0]
    q_rep = jnp.concatenate([q] * n_heads, axis=0)
    row_h = lax.broadcasted_iota(jnp.int32, (rows, width), 0) // t_q
    col_h = lax.broadcasted_iota(jnp.int32, (rows, width), 1) // HEAD_DIM
    diag = row_h == col_h
    qmt = jnp.where(diag, q_rep, 0.0).astype(bf16)

    lane = lax.broadcasted_iota(jnp.int32, (rows, LANES), 1)
    lane_f = lane.astype(f32)

    def phase1(i, gate):
        slot = i % S_NBUF
        nxt = i + S_AHEAD
        nslot = nxt % S_NBUF

        @pl.when(nxt < n_past_blk)
        def _():
            start_blk(ck_hbm, b, nxt, nslot)

        @pl.when(nxt >= n_past_blk)
        def _():
            start_blk(cv_hbm, b, nxt - n_past_blk, nslot)

        wait_blk(slot)
        s_t = jnp.concatenate(
            [jnp.dot(qmt, buf[slot, pg].astype(bf16), preferred_element_type=f32)
             for pg in range(PAGES_PER_BLK)], axis=1)
        s_scr[i] = s_t
        bsum = jnp.sum(s_t, axis=1, keepdims=True)
        return jnp.where(lane == i, bsum, gate)

    gate = lax.fori_loop(0, n_past_blk, phase1, jnp.zeros((rows, LANES), f32))
    gate = gate * (1.0 / MOBA_BLOCK)

    own = n_past_blk
    gv = jnp.where(lane < own, gate, jnp.where(lane == own, NEG, -jnp.inf))
    picked = jnp.zeros(gv.shape, jnp.bool_)
    for _ in range(min(MOBA_TOPK, n_past_blk + 1)):
        mx = jnp.max(gv, axis=1, keepdims=True)
        idx = jnp.min(jnp.where(gv == mx, lane_f, float(LANES)), axis=1, keepdims=True)
        hit = lane_f == idx
        picked = picked | hit
        gv = jnp.where(hit, -jnp.inf, gv)
    selneg = jnp.where(picked & (lane < own), 0.0, NEG)

    scale = HEAD_DIM ** -0.5
    c31 = c31_ref[...]

    def pass_a(i, mrun):
        col = jnp.sum(jnp.where(lane == i, selneg, 0.0), axis=1, keepdims=True)
        bias = jnp.where(i == n_past_blk - 1, near_ref[...], jnp.broadcast_to(c31, (rows, MOBA_BLOCK)))
        logit = s_scr[i] * scale + bias + col
        s_scr[i] = logit
        return jnp.maximum(mrun, jnp.maximum(logit[:, :LANES], logit[:, LANES:]))

    mrun = lax.fori_loop(0, n_past_blk, pass_a, jnp.full((rows, LANES), -jnp.inf, f32))

    kn = jnp.concatenate([kn_ref[0], jnp.zeros((LANES - t_q, width), f32)], axis=0).astype(bf16)
    vn = jnp.concatenate([vn_ref[0], jnp.zeros((LANES - t_q, width), f32)], axis=0).astype(bf16)
    logit_own = lax.dot_general(qmt, kn, _NT, preferred_element_type=f32) * scale + ownb_ref[...]
    mrun = jnp.maximum(mrun, logit_own)
    m_row = jnp.max(mrun, axis=1, keepdims=True)

    p_own = jnp.exp(logit_own - m_row)
    acc0 = jnp.dot(p_own.astype(bf16), vn, preferred_element_type=f32)

    def pass_b(i, carry):
        acc, psum = carry
        slot = i % S_NBUF
        nxt = i + S_AHEAD
        nslot = nxt % S_NBUF

        @pl.when(nxt < n_past_blk)
        def _():
            start_blk(cv_hbm, b, nxt, nslot)

        @pl.when((nxt >= n_past_blk) & (b + 1 < n_seq))
        def _():
            start_blk(ck_hbm, b + 1, nxt - n_past_blk, nslot)

        wait_blk(slot)
        p = jnp.exp(s_scr[i] - m_row)
        pb = p.astype(bf16)
        for pg in range(PAGES_PER_BLK):
            vt = buf[slot, pg].astype(bf16)
            acc = acc + lax.dot_general(pb[:, pg * page:(pg + 1) * page], vt, _NT, preferred_element_type=f32)
        return acc, psum + p[:, :LANES] + p[:, LANES:]

    acc, psum = lax.fori_loop(0, n_past_blk, pass_b, (acc0, p_own))
    l_row = jnp.sum(psum, axis=1, keepdims=True)
    o_full = jnp.where(diag, acc / l_row, 0.0)
    o_ref[0] = jnp.sum(o_full.reshape(n_heads, t_q, width), axis=0)


def _s_attn(page_table, q, k_new, v_new, near_b, own_b, c31, cache_k, cache_v):
    n_seq, t_q, width = q.shape
    n_pages = page_table.shape[1]
    page = cache_k.shape[2]
    assert page * PAGES_PER_BLK == MOBA_BLOCK and page == LANES and cache_k.shape[1] == width
    n_past_blk = n_pages // PAGES_PER_BLK
    assert n_past_blk >= S_NBUF and n_past_blk % S_NBUF == 0 and n_past_blk < LANES
    rows = (width // HEAD_DIM) * t_q
    per_seq = lambda b, pt: (b, 0, 0)
    whole = lambda b, pt: (0, 0)
    grid_spec = pltpu.PrefetchScalarGridSpec(
        num_scalar_prefetch=1,
        grid=(n_seq,),
        in_specs=[pl.BlockSpec((1, t_q, width), per_seq),
                  pl.BlockSpec((1, t_q, width), per_seq),
                  pl.BlockSpec((1, t_q, width), per_seq),
                  pl.BlockSpec((rows, MOBA_BLOCK), whole),
                  pl.BlockSpec((rows, LANES), whole),
                  pl.BlockSpec((rows, 1), whole),
                  pl.BlockSpec(memory_space=pl.ANY),
                  pl.BlockSpec(memory_space=pl.ANY)],
        out_specs=pl.BlockSpec((1, t_q, width), per_seq),
        scratch_shapes=[pltpu.VMEM((S_NBUF, PAGES_PER_BLK, width, page), f32),
                        pltpu.SemaphoreType.DMA((S_NBUF,)),
                        pltpu.VMEM((n_past_blk, rows, MOBA_BLOCK), f32)],
    )
    return pl.pallas_call(
        functools.partial(_s_attn_kernel, n_seq=n_seq, n_pages=n_pages, n_past_blk=n_past_blk, page=page),
        out_shape=jax.ShapeDtypeStruct((n_seq, t_q, width), f32),
        grid_spec=grid_spec,
        compiler_params=pltpu.CompilerParams(dimension_semantics=("arbitrary",),
                                             vmem_limit_bytes=VMEM_LIMIT),
        name="s_attn",
    )(page_table.reshape(-1), q, k_new, v_new, near_b, own_b, c31, cache_k, cache_v)


def _merge_tail(o, za, conv, zc, lng_ref, lnb_ref, wo_ref, x, gate, gf_ref):
    mu = jnp.mean(conv, axis=-1, keepdims=True)
    xc = conv - mu
    var = jnp.mean(xc * xc, axis=-1, keepdims=True)
    ln = xc * lax.rsqrt(var + EPS) * lng_ref[...] + lnb_ref[...]
    y_conv = _silu(ln) * zc
    y_attn = o * za
    merged = jnp.concatenate([y_attn, y_conv], axis=-1).astype(bf16)
    proj = jnp.dot(merged, wo_ref[...], preferred_element_type=f32)
    xs = x + gate * proj
    ms = jnp.mean(xs * xs, axis=-1, keepdims=True)
    return xs * lax.rsqrt(ms + EPS) * gf_ref[...]


HALO = 32


def _merge_p_kernel(o_ref, za_ref, u_ref, uh_ref, zc_ref, cw_ref, cb_ref, lng_ref, lnb_ref, wo_ref,
                    x_ref, gate_ref, gf_ref, y_ref, ext):
    i = pl.program_id(1)
    tm = u_ref.shape[1]
    ext[pl.ds(0, HALO), :] = jnp.where(i == 0, 0.0, uh_ref[0])
    ext[pl.ds(HALO, tm), :] = u_ref[0]
    conv = jnp.broadcast_to(cb_ref[...], (tm, u_ref.shape[2]))
    for j in range(CONV_SPAN):
        conv = conv + cw_ref[j:j + 1, :] * ext[pl.ds(HALO - CONV_CTX + j, tm), :]
    y_ref[0] = _merge_tail(o_ref[0], za_ref[0], conv, zc_ref[0], lng_ref, lnb_ref, wo_ref,
                           x_ref[0], gate_ref[0], gf_ref)


def _merge_s_kernel(o_ref, za_ref, u_ref, st_ref, zc_ref, cw_ref, cb_ref, lng_ref, lnb_ref, wo_ref,
                    x_ref, gate_ref, gf_ref, y_ref, ext):
    g_sz, t_q, width = u_ref.shape
    d = x_ref.shape[2]
    ext[:, pl.ds(0, CONV_CTX), :] = st_ref[...]
    ext[:, pl.ds(CONV_CTX, t_q), :] = u_ref[...]
    conv = jnp.broadcast_to(cb_ref[...].reshape(1, 1, width), (g_sz, t_q, width))
    for j in range(CONV_SPAN):
        conv = conv + cw_ref[j:j + 1, :].reshape(1, 1, width) * ext[:, pl.ds(j, t_q), :]
    rows = g_sz * t_q
    gate = jnp.broadcast_to(gate_ref[...], (g_sz, t_q, d)).reshape(rows, d)
    y = _merge_tail(o_ref[...].reshape(rows, width), za_ref[...].reshape(rows, width),
                    conv.reshape(rows, width), zc_ref[...].reshape(rows, width),
                    lng_ref, lnb_ref, wo_ref, x_ref[...].reshape(rows, d), gate, gf_ref)
    y_ref[...] = y.reshape(g_sz, t_q, d)


def _merge_p(o, za, u, zc, conv_w, conv_b, ln_g, ln_b, w_out_b, x, gate, g_final, *, tm):
    n_b, seq, width = u.shape
    d = x.shape[2]
    row = lambda b, i: (b, i, 0)
    const2 = lambda b, i: (0, 0)
    halo_blocks = tm // HALO
    return pl.pallas_call(
        _merge_p_kernel,
        out_shape=jax.ShapeDtypeStruct((n_b, seq, d), f32),
        grid=(n_b, seq // tm),
        in_specs=[pl.BlockSpec((1, tm, width), row),
                  pl.BlockSpec((1, tm, width), row),
                  pl.BlockSpec((1, tm, width), row),
                  pl.BlockSpec((1, HALO, width), lambda b, i: (b, jnp.maximum(i * halo_blocks - 1, 0), 0)),
                  pl.BlockSpec((1, tm, width), row),
                  pl.BlockSpec(conv_w.shape, const2),
                  pl.BlockSpec((1, width), const2),
                  pl.BlockSpec((1, width), const2),
                  pl.BlockSpec((1, width), const2),
                  pl.BlockSpec(w_out_b.shape, const2),
                  pl.BlockSpec((1, tm, d), row),
                  pl.BlockSpec((1, 1, d), lambda b, i: (b, 0, 0)),
                  pl.BlockSpec((1, d), const2)],
        out_specs=pl.BlockSpec((1, tm, d), row),
        scratch_shapes=[pltpu.VMEM((HALO + tm, width), f32)],
        compiler_params=pltpu.CompilerParams(dimension_semantics=("arbitrary", "arbitrary"),
                                             vmem_limit_bytes=VMEM_LIMIT),
        name="merge_p",
    )(o, za, u, u, zc, conv_w, conv_b.reshape(1, width), ln_g.reshape(1, width), ln_b.reshape(1, width),
      w_out_b, x, gate.reshape(n_b, 1, d), g_final.reshape(1, d))


def _merge_s(o, za, u, state, zc, conv_w, conv_b, ln_g, ln_b, w_out_b, x, gate, g_final, *, g_sz):
    n_b, t_q, width = u.shape
    d = x.shape[2]
    grp = lambda b: (b, 0, 0)
    const2 = lambda b: (0, 0)
    ext_rows = -(-(CONV_CTX + t_q) // SUBLANES) * SUBLANES
    return pl.pallas_call(
        _merge_s_kernel,
        out_shape=jax.ShapeDtypeStruct((n_b, t_q, d), f32),
        grid=(n_b // g_sz,),
        in_specs=[pl.BlockSpec((g_sz, t_q, width), grp),
                  pl.BlockSpec((g_sz, t_q, width), grp),
                  pl.BlockSpec((g_sz, t_q, width), grp),
                  pl.BlockSpec((g_sz, CONV_CTX, width), grp),
                  pl.BlockSpec((g_sz, t_q, width), grp),
                  pl.BlockSpec(conv_w.shape, const2),
                  pl.BlockSpec((1, width), const2),
                  pl.BlockSpec((1, width), const2),
                  pl.BlockSpec((1, width), const2),
                  pl.BlockSpec(w_out_b.shape, const2),
                  pl.BlockSpec((g_sz, t_q, d), grp),
                  pl.BlockSpec((g_sz, 1, d), grp),
                  pl.BlockSpec((1, d), const2)],
        out_specs=pl.BlockSpec((g_sz, t_q, d), grp),
        scratch_shapes=[pltpu.VMEM((g_sz, ext_rows, width), f32)],
        compiler_params=pltpu.CompilerParams(dimension_semantics=("arbitrary",),
                                             vmem_limit_bytes=VMEM_LIMIT),
        name="merge_s",
    )(o, za, u, state, zc, conv_w, conv_b.reshape(1, width), ln_g.reshape(1, width), ln_b.reshape(1, width),
      w_out_b, x, gate.reshape(n_b, 1, d), g_final.reshape(1, d))


def _bias_index_tables(t_q, n_past):
    i = np.arange(Q_CHUNK)[:, None]
    jj = np.arange(MOBA_BLOCK)[None, :]
    tiles = [_bucket_np(i - jj), _bucket_np(i + Q_CHUNK - jj), _bucket_np(i + MOBA_BLOCK - jj),
             np.full((Q_CHUNK, MOBA_BLOCK), N_BUCKETS - 1, np.int32)]
    it = np.arange(t_q)[:, None]
    near = _bucket_np(n_past + it - ((n_past // MOBA_BLOCK - 1) * MOBA_BLOCK + jj))
    own = np.where(jj < t_q, _bucket_np(it - jj), -1)
    return np.concatenate(tiles + [near, own], axis=0).astype(np.int32)


def kernel(x_prompt, x_sample, cache_k, cache_v, state_conv, page_table, c_prompt, c_sample,
           w_ada, b_ada, g_norm, w_in, conv_w, conv_b, ln_g, ln_b, w_out, rel_bias, g_final):
    depth = w_in.shape[0]
    assert depth == 1
    n_b, seq, d = x_prompt.shape
    n_s, t_q, _ = x_sample.shape
    n_heads = rel_bias.shape[1]
    width = n_heads * HEAD_DIM
    page = cache_k.shape[2]
    n_past = page_table.shape[1] * page
    assert seq % MOBA_BLOCK == 0 and n_past % MOBA_BLOCK == 0 and t_q <= SUBLANES and seq >= CONV_CTX
    l = 0

    n_c = n_b + n_s
    c_all = jnp.concatenate([c_prompt, c_sample], axis=0)
    mod = _ada(c_all, w_ada[l], b_ada[l])
    shift, scale, gate = mod[:, :d], mod[:, d:2 * d], mod[:, 2 * d:]

    idx = _bias_index_tables(t_q, n_past)
    bias_all = _bias_tiles(rel_bias, idx)
    n_ptiles = 4
    bias_p = bias_all[:, :n_ptiles * Q_CHUNK].reshape(n_heads, n_ptiles, Q_CHUNK, MOBA_BLOCK)
    near_b = bias_all[:, n_ptiles * Q_CHUNK:n_ptiles * Q_CHUNK + t_q].reshape(n_heads * t_q, MOBA_BLOCK)
    own_b = bias_all[:, n_ptiles * Q_CHUNK + t_q:, :LANES].reshape(n_heads * t_q, LANES)
    c31 = jnp.repeat(bias_all[:, n_ptiles * Q_CHUNK - 1, 0], t_q).reshape(n_heads * t_q, 1)

    w_in_b = w_in[l].astype(bf16)
    w_out_b = w_out[l].astype(bf16)

    q, k, v, za, u, zc = _in_proj(x_prompt, scale[:n_b], shift[:n_b], g_norm[l], w_in_b, g_sz=1, tm=256)
    o = _p_attn(q, k, v, bias_p)
    y_prompt = _merge_p(o, za, u, zc, conv_w[l], conv_b[l], ln_g[l], ln_b[l], w_out_b,
                        x_prompt, gate[:n_b], g_final, tm=256)
    k_prompt = k.reshape(1, n_b, seq, n_heads, HEAD_DIM)
    v_prompt = v.reshape(1, n_b, seq, n_heads, HEAD_DIM)
    conv_prompt = u[:, seq - CONV_CTX:, :][None]

    qs, ks, vs, zas, us, zcs = _in_proj(x_sample, scale[n_b:], shift[n_b:], g_norm[l], w_in_b, g_sz=32, tm=t_q)
    ck = jnp.transpose(cache_k[l], (0, 2, 3, 1)).reshape(cache_k.shape[1], width, page)
    cv = jnp.transpose(cache_v[l], (0, 2, 3, 1)).reshape(cache_v.shape[1], width, page)
    os_ = _s_attn(page_table, qs, ks, vs, near_b, own_b, c31, ck, cv)
    y_sample = _merge_s(os_, zas, us, state_conv[l], zcs, conv_w[l], conv_b[l], ln_g[l], ln_b[l], w_out_b,
                        x_sample, gate[n_b:], g_final, g_sz=32)
    k_sample = ks.reshape(1, n_s, t_q, n_heads, HEAD_DIM)
    v_sample = vs.reshape(1, n_s, t_q, n_heads, HEAD_DIM)
    conv_sample = jnp.concatenate([state_conv[l][:, t_q:, :], us], axis=1)[None]

    return (y_prompt, y_sample, k_prompt, v_prompt, conv_prompt, k_sample, v_sample, conv_sample)
```
